```python
import math
import jax
import jax.numpy as jnp
from jax import lax
import numpy as np

D_MODEL = 1024
BATCH = 8
SEQ = 4096
DEPTH = 1
DEC_BATCH = 32
DEC_SEQ = 4
PAST_LEN = 16384
PAGE_SIZE = 128

A_HEADS = 8
A_HEAD_DIM = 64
A_WIDTH = A_HEADS * A_HEAD_DIM
MOBA_BLOCK = 256
MOBA_TOPK = 3
Q_BLOCK = 128
REL_BUCKETS = 32
REL_MAX_DIST = 128
G_HEADS = 4
G_KEY_DIM = D_MODEL // 2
G_VAL_DIM = D_MODEL
G_DK = G_KEY_DIM // G_HEADS
G_DV = G_VAL_DIM // G_HEADS
G_GATE_RANK = 16
G_GATE_NORMALIZER = 16.0
G_CHUNK = 32
N_GROUPS = 4
EXPERTS_PER_GROUP = 8
N_EXPERTS = N_GROUPS * EXPERTS_PER_GROUP
EXPERT_TOPK = 2
EXPERT_HIDDEN = 512
MOE_BLOCK = 128
EPS = 1e-6

IN_SIZES = (A_WIDTH, A_WIDTH, A_WIDTH, G_KEY_DIM, G_KEY_DIM, G_VAL_DIM, G_VAL_DIM, G_GATE_RANK, D_MODEL, D_MODEL)
IN_WIDTH = sum(IN_SIZES)

kernel_name = 'hybrid_moba_gla_hmoe_decode_step'


def _in_offsets():
    return [int(o) for o in np.cumsum(IN_SIZES)[:-1]]


def rms_norm(x, g):
    xf = x.astype(jnp.float32)
    y = xf * lax.rsqrt(jnp.mean(xf * xf, axis=-1, keepdims=True) + EPS)
    return (y * g.astype(jnp.float32)).astype(x.dtype)


def t5_bucket(dist):
    n = jnp.maximum(dist, 0)
    max_exact = REL_BUCKETS // 2
    nf = jnp.maximum(n, 1).astype(jnp.float32)
    large = max_exact + (jnp.log(nf / max_exact) / math.log(REL_MAX_DIST / max_exact)
                         * (REL_BUCKETS - max_exact)).astype(jnp.int32)
    large = jnp.minimum(large, REL_BUCKETS - 1)
    return jnp.where(n < max_exact, n, large)


def moba_sequence(q, k, v, rel_bias, offset):
    tq = q.shape[0]
    tk = k.shape[0]
    n_blocks = -(-tk // MOBA_BLOCK)
    pad = n_blocks * MOBA_BLOCK - tk
    kbh = jnp.pad(k, ((0, pad), (0, 0), (0, 0))).reshape(n_blocks, MOBA_BLOCK, A_HEADS, A_HEAD_DIM).transpose(2, 0, 1, 3)
    vbh = jnp.pad(v, ((0, pad), (0, 0), (0, 0))).reshape(n_blocks, MOBA_BLOCK, A_HEADS, A_HEAD_DIM).transpose(2, 0, 1, 3)
    k_mean = jnp.mean(kbh.astype(jnp.float32), axis=2)
    rb_t = rel_bias.T.astype(jnp.float32)
    n_sel = min(MOBA_TOPK, n_blocks)
    qc = math.gcd(tq, Q_BLOCK)
    head = jnp.arange(A_HEADS)
    blk = jnp.arange(MOBA_BLOCK)
    scale = A_HEAD_DIM ** -0.5

    def chunk(ci):
        start = ci * qc
        q_c = lax.dynamic_slice_in_dim(q, start, qc, 0).transpose(1, 0, 2)
        pos_q = offset + start + jnp.arange(qc)
        n_past = pos_q // MOBA_BLOCK
        own = (offset + start) // MOBA_BLOCK
        gate = jnp.einsum('hqd,hnd->hqn', q_c, k_mean, preferred_element_type=jnp.float32)
        is_past = jnp.arange(n_blocks)[None, None, :] < n_past[None, :, None]
        _, sel = lax.top_k(jnp.where(is_past, gate, -jnp.inf), n_sel)
        sel_ok = jnp.arange(n_sel)[None, :] < n_past[:, None]
        k_sel = kbh[head[:, None, None], sel]
        v_sel = vbh[head[:, None, None], sel]
        key_pos_sel = sel[..., None] * MOBA_BLOCK + blk
        bias_sel = rb_t[head[:, None, None, None], t5_bucket(pos_q[None, :, None, None] - key_pos_sel)]
        logit_sel = jnp.einsum('hqd,hqnkd->hqnk', q_c, k_sel, preferred_element_type=jnp.float32) * scale + bias_sel
        logit_sel = jnp.where(sel_ok[None, :, :, None], logit_sel, -jnp.inf)
        k_own = lax.dynamic_index_in_dim(kbh, own, axis=1, keepdims=False)
        v_own = lax.dynamic_index_in_dim(vbh, own, axis=1, keepdims=False)
        dist_own = pos_q[:, None] - (own * MOBA_BLOCK + blk)[None, :]
        logit_own = jnp.einsum('hqd,hkd->hqk', q_c, k_own, preferred_element_type=jnp.float32) * scale + rb_t[:, t5_bucket(dist_own)]
        logit_own = jnp.where(dist_own[None] >= 0, logit_own, -jnp.inf)
        logits = jnp.concatenate([logit_sel.reshape(A_HEADS, qc, n_sel * MOBA_BLOCK), logit_own], axis=-1)
        p = jax.nn.softmax(logits, axis=-1)
        p_sel = p[..., : n_sel * MOBA_BLOCK].reshape(A_HEADS, qc, n_sel, MOBA_BLOCK).astype(v.dtype)
        p_own = p[..., n_sel * MOBA_BLOCK:].astype(v.dtype)
        out = jnp.einsum('hqnk,hqnkd->hqd', p_sel, v_sel) + jnp.einsum('hqk,hkd->hqd', p_own, v_own)
        return out.transpose(1, 0, 2)

    out = lax.map(chunk, jnp.arange(tq // qc))
    return out.reshape(tq, A_HEADS, A_HEAD_DIM)


def gla_chunked(q, k, v, log_a, s0):
    b, t = q.shape[:2]
    c = math.gcd(t, G_CHUNK)
    n = t // c

    def to_chunks(z):
        return z.astype(jnp.float32).reshape(b, n, c, *z.shape[2:]).swapaxes(0, 1)

    causal = jnp.tril(jnp.ones((c, c), dtype=bool))

    def step(s, inp):
        q_c, k_c, v_c, a_c = inp
        cum = jnp.cumsum(a_c, axis=1)
        q_t = q_c * jnp.exp(cum)
        k_t = k_c * jnp.exp(-cum)
        inter = jnp.einsum('bthk,bhkv->bthv', q_t, s)
        att = jnp.where(causal, jnp.einsum('bthk,bshk->bhts', q_t, k_t), 0.0)
        intra = jnp.einsum('bhts,bshv->bthv', att, v_c)
        last = cum[:, -1]
        k_end = k_c * jnp.exp(last[:, None] - cum)
        s_new = s * jnp.exp(last)[..., None] + jnp.einsum('bshk,bshv->bhkv', k_end, v_c)
        return s_new, inter + intra

    s_fin, o = lax.scan(step, s0.astype(jnp.float32), (to_chunks(q), to_chunks(k), to_chunks(v), to_chunks(log_a)))
    o = o.swapaxes(0, 1).reshape(b, t, G_HEADS, G_DV)
    return o, s_fin.astype(s0.dtype)


def hier_moe(h, w_router_group, b_router_group, w_router_expert, b_router_expert, w_exp_gate, w_exp_up, w_exp_down):
    b, t, d = h.shape
    hf = h.reshape(-1, d)
    n = hf.shape[0]
    g_logits = (hf @ w_router_group + b_router_group).astype(jnp.float32)
    grp = jnp.argmax(g_logits, axis=-1)
    p_grp = jnp.take_along_axis(jax.nn.softmax(g_logits, axis=-1), grp[:, None], axis=-1)
    e_logits = (hf @ w_router_expert + b_router_expert).astype(jnp.float32).reshape(n, N_GROUPS, EXPERTS_PER_GROUP)
    e_in = jnp.take_along_axis(e_logits, grp[:, None, None], axis=1)[:, 0]
    top_val, top_idx = lax.top_k(e_in, EXPERT_TOPK)
    weights = p_grp * jax.nn.softmax(top_val, axis=-1)
    expert = grp[:, None] * EXPERTS_PER_GROUP + top_idx

    n_assign = n * EXPERT_TOPK
    flat_e = expert.reshape(-1)
    order = jnp.argsort(flat_e)
    se = flat_e[order]
    sw = weights.reshape(-1)[order]
    stok = jnp.repeat(jnp.arange(n, dtype=jnp.int32), EXPERT_TOPK)[order]
    counts = jnp.bincount(flat_e, length=N_EXPERTS)
    starts = jnp.cumsum(counts) - counts
    padded = (counts + MOE_BLOCK - 1) // MOE_BLOCK * MOE_BLOCK
    pad_ends = jnp.cumsum(padded)
    pad_starts = pad_ends - padded
    dest = pad_starts[se] + (jnp.arange(n_assign) - starts[se])
    n_rows = (n_assign + N_EXPERTS * (MOE_BLOCK - 1) + MOE_BLOCK - 1) // MOE_BLOCK * MOE_BLOCK
    n_blk = n_rows // MOE_BLOCK
    row_tok = jnp.full((n_rows,), n, dtype=jnp.int32).at[dest].set(stok)
    row_w = jnp.zeros((n_rows,), dtype=jnp.float32).at[dest].set(sw)
    blk_expert = jnp.minimum(jnp.searchsorted(pad_ends, jnp.arange(n_blk) * MOE_BLOCK, side='right'), N_EXPERTS - 1)
    h_pad = jnp.concatenate([hf, jnp.zeros((1, d), hf.dtype)], axis=0)

    def block(args):
        toks, wts, e = args
        xb = h_pad[toks]
        a = xb @ w_exp_gate[e]
        u = xb @ w_exp_up[e]
        return ((jax.nn.silu(a) * u) @ w_exp_down[e]) * wts[:, None].astype(xb.dtype)

    y_rows = lax.map(block, (row_tok.reshape(n_blk, MOE_BLOCK), row_w.reshape(n_blk, MOE_BLOCK), blk_expert))
    y = jax.ops.segment_sum(y_rows.reshape(n_rows, d), row_tok, num_segments=n + 1)[:n]
    return y.reshape(b, t, d)


def decoder_layer(x, c, attend, s0, norm1_g, norm2_g, w_ada, b_ada, w_in, w_gk2, b_gk,
                  q_norm_g, k_norm_g, o_norm_g, w_branch_a, w_branch_b, w_out,
                  w_router_group, b_router_group, w_router_expert, b_router_expert,
                  w_exp_gate, w_exp_up, w_exp_down):
    b, t, _ = x.shape
    mod = jax.nn.silu(c) @ w_ada + b_ada
    shift1, scale1, gate1, shift2, scale2, gate2 = jnp.split(mod[:, None, :], 6, axis=-1)
    h = rms_norm(x, norm1_g) * (1 + scale1) + shift1
    q_a, k_a, v_a, q_g, k_g, v_g, og_g, gk_lr, gate_a, gate_b = jnp.split(h @ w_in, _in_offsets(), axis=-1)
    q_a = rms_norm(q_a.reshape(b, t, A_HEADS, A_HEAD_DIM), q_norm_g)
    k_a = rms_norm(k_a.reshape(b, t, A_HEADS, A_HEAD_DIM), k_norm_g)
    v_a = v_a.reshape(b, t, A_HEADS, A_HEAD_DIM)
    o_a = attend(q_a, k_a, v_a).reshape(b, t, A_WIDTH)
    log_a = jax.nn.log_sigmoid((gk_lr @ w_gk2 + b_gk).astype(jnp.float32)) / G_GATE_NORMALIZER
    o_g, s_new = gla_chunked(q_g.reshape(b, t, G_HEADS, G_DK) * (G_DK ** -0.5),
                             k_g.reshape(b, t, G_HEADS, G_DK),
                             v_g.reshape(b, t, G_HEADS, G_DV),
                             log_a.reshape(b, t, G_HEADS, G_DK), s0)
    o_g = rms_norm(o_g.astype(x.dtype), o_norm_g) * jax.nn.silu(og_g).reshape(b, t, G_HEADS, G_DV)
    o_g = o_g.reshape(b, t, G_VAL_DIM)
    merged = jax.nn.sigmoid(gate_a) * (o_a @ w_branch_a) + jax.nn.sigmoid(gate_b) * (o_g @ w_branch_b)
    x = x + gate1 * (merged @ w_out)
    h2 = rms_norm(x, norm2_g) * (1 + scale2) + shift2
    x = x + gate2 * hier_moe(h2, w_router_group, b_router_group, w_router_expert, b_router_expert,
                             w_exp_gate, w_exp_up, w_exp_down)
    return x, k_a, v_a, s_new


def setup_inputs(seed: int = 0) -> dict:
    key = jax.random.key(seed)
    ks = list(jax.random.split(key, 32))
    n_pages = PAST_LEN // PAGE_SIZE
    n_pool = (DEC_BATCH * n_pages * 5) // 4
    f32 = jnp.float32

    def nrm(k, shape, s=1.0):
        return jax.random.normal(k, shape, f32) * s

    def gain(k, shape):
        return 1.0 + nrm(k, shape, 0.02)

    page_table = jax.random.permutation(ks[7], n_pool)[: DEC_BATCH * n_pages].reshape(DEC_BATCH, n_pages).astype(jnp.int32)
    return {
        'x_prompt': nrm(ks[0], (BATCH, SEQ, D_MODEL)),
        'x_sample': nrm(ks[1], (DEC_BATCH, DEC_SEQ, D_MODEL)),
        'c_prompt': nrm(ks[2], (BATCH, D_MODEL)),
        'c_sample': nrm(ks[3], (DEC_BATCH, D_MODEL)),
        'cache_k': nrm(ks[4], (DEPTH, n_pool, PAGE_SIZE, A_HEADS, A_HEAD_DIM)),
        'cache_v': nrm(ks[5], (DEPTH, n_pool, PAGE_SIZE, A_HEADS, A_HEAD_DIM)),
        'state_gla': nrm(ks[6], (DEPTH, DEC_BATCH, G_HEADS, G_DK, G_DV)),
        'page_table': page_table,
        'rel_bias': nrm(ks[8], (REL_BUCKETS, A_HEADS), 0.5),
        'norm1_g': gain(ks[9], (DEPTH, D_MODEL)),
        'norm2_g': gain(ks[10], (DEPTH, D_MODEL)),
        'w_ada': nrm(ks[11], (DEPTH, D_MODEL, 6 * D_MODEL), 0.5 * D_MODEL ** -0.5),
        'b_ada': nrm(ks[12], (DEPTH, 6 * D_MODEL), 0.01),
        'w_in': nrm(ks[13], (DEPTH, D_MODEL, IN_WIDTH), D_MODEL ** -0.5),
        'w_gk2': nrm(ks[14], (DEPTH, G_GATE_RANK, G_KEY_DIM), G_GATE_RANK ** -0.5),
        'b_gk': nrm(ks[15], (DEPTH, G_KEY_DIM), 0.01),
        'q_norm_g': gain(ks[16], (DEPTH, A_HEAD_DIM)),
        'k_norm_g': gain(ks[17], (DEPTH, A_HEAD_DIM)),
        'o_norm_g': gain(ks[18], (DEPTH, G_DV)),
        'w_branch_a': nrm(ks[19], (DEPTH, A_WIDTH, D_MODEL), A_WIDTH ** -0.5),
        'w_branch_b': nrm(ks[20], (DEPTH, G_VAL_DIM, D_MODEL), G_VAL_DIM ** -0.5),
        'w_out': nrm(ks[21], (DEPTH, D_MODEL, D_MODEL), D_MODEL ** -0.5),
        'w_router_group': nrm(ks[22], (DEPTH, D_MODEL, N_GROUPS), D_MODEL ** -0.5),
        'b_router_group': nrm(ks[23], (DEPTH, N_GROUPS), 0.01),
        'w_router_expert': nrm(ks[24], (DEPTH, D_MODEL, N_EXPERTS), D_MODEL ** -0.5),
        'b_router_expert': nrm(ks[25], (DEPTH, N_EXPERTS), 0.01),
        'w_exp_gate': nrm(ks[26], (DEPTH, N_EXPERTS, D_MODEL, EXPERT_HIDDEN), D_MODEL ** -0.5),
        'w_exp_up': nrm(ks[27], (DEPTH, N_EXPERTS, D_MODEL, EXPERT_HIDDEN), D_MODEL ** -0.5),
        'w_exp_down': nrm(ks[28], (DEPTH, N_EXPERTS, EXPERT_HIDDEN, D_MODEL), EXPERT_HIDDEN ** -0.5),
    }


def reference(x_prompt, x_sample, c_prompt, c_sample, cache_k, cache_v, state_gla, page_table,
              rel_bias, norm1_g, norm2_g, w_ada, b_ada, w_in, w_gk2, b_gk, q_norm_g, k_norm_g, o_norm_g,
              w_branch_a, w_branch_b, w_out, w_router_group, b_router_group, w_router_expert, b_router_expert,
              w_exp_gate, w_exp_up, w_exp_down):
    n_pages = page_table.shape[1]
    past_len = n_pages * cache_k.shape[2]

    def attend_prompt(q, k, v):
        return lax.map(lambda a: moba_sequence(a[0], a[1], a[2], rel_bias, 0), (q, k, v))

    def make_attend_sample(layer):
        def attend_sample(q, k, v):
            def one(a):
                qb, kn, vn, pages = a
                k_past = cache_k[layer, pages].reshape(past_len, A_HEADS, A_HEAD_DIM)
                v_past = cache_v[layer, pages].reshape(past_len, A_HEADS, A_HEAD_DIM)
                k_full = jnp.concatenate([k_past, kn.astype(k_past.dtype)], axis=0)
                v_full = jnp.concatenate([v_past, vn.astype(v_past.dtype)], axis=0)
                return moba_sequence(qb, k_full, v_full, rel_bias, past_len)
            return lax.map(one, (q, k, v, page_table))
        return attend_sample

    y_p, y_s = x_prompt, x_sample
    kp_list, vp_list, sp_list, ks_list, vs_list, ss_list = [], [], [], [], [], []
    for layer in range(DEPTH):
        lw = (norm1_g[layer], norm2_g[layer], w_ada[layer], b_ada[layer], w_in[layer], w_gk2[layer], b_gk[layer],
              q_norm_g[layer], k_norm_g[layer], o_norm_g[layer], w_branch_a[layer], w_branch_b[layer], w_out[layer],
              w_router_group[layer], b_router_group[layer], w_router_expert[layer], b_router_expert[layer],
              w_exp_gate[layer], w_exp_up[layer], w_exp_down[layer])
        s0_prompt = jnp.zeros((x_prompt.shape[0], G_HEADS, G_DK, G_DV), state_gla.dtype)
        y_p, k_p, v_p, s_p = decoder_layer(y_p, c_prompt, attend_prompt, s0_prompt, *lw)
        y_s, k_s, v_s, s_s = decoder_layer(y_s, c_sample, make_attend_sample(layer), state_gla[layer], *lw)
        kp_list.append(k_p)
        vp_list.append(v_p)
        sp_list.append(s_p)
        ks_list.append(k_s)
        vs_list.append(v_s)
        ss_list.append(s_s)
    k_prompt = jnp.stack(kp_list)
    v_prompt = jnp.stack(vp_list)
    state_gla_prompt = jnp.stack(sp_list)
    k_sample = jnp.stack(ks_list)
    v_sample = jnp.stack(vs_list)
    state_gla_sample = jnp.stack(ss_list)
    return (y_p, y_s, k_prompt, v_prompt, state_gla_prompt, k_sample, v_sample, state_gla_sample)
```

```python
import functools
import math

import numpy as np
import jax
import jax.numpy as jnp
from jax import lax
from jax.experimental import pallas as pl
from jax.experimental.pallas import tpu as pltpu

F32 = jnp.float32
BF16 = jnp.bfloat16
I32 = jnp.int32

D_MODEL = 1024
A_HEADS = 8
A_HEAD_DIM = 64
A_WIDTH = A_HEADS * A_HEAD_DIM
MOBA_BLOCK = 256
MOBA_TOPK = 3
REL_BUCKETS = 32
REL_MAX_DIST = 128
G_HEADS = 4
G_DK = 128
G_DV = 256
G_KEY_DIM = G_HEADS * G_DK
G_VAL_DIM = G_HEADS * G_DV
G_GATE_RANK = 16
G_GATE_NORMALIZER = 16.0
N_GROUPS = 4
EXPERTS_PER_GROUP = 8
N_EXPERTS = N_GROUPS * EXPERTS_PER_GROUP
EXPERT_TOPK = 2
EXPERT_HIDDEN = 512
MOE_BLOCK = 128
EPS = 1e-6

LANES = 128
NEG = -1e30
VMEM_LIMIT = 56 * 1024 * 1024
GLA_CHUNK = 64
PAGES_PER_STEP = 8

_C_QKV = 3 * A_WIDTH
_C_QKG = _C_QKV + 2 * G_KEY_DIM
_C_VG = _C_QKG + G_VAL_DIM
_C_OG = _C_VG + G_VAL_DIM
_C_GK = _C_OG + LANES


def _cparams(sem):
    return pltpu.CompilerParams(dimension_semantics=sem, vmem_limit_bytes=VMEM_LIMIT)


def _silu(x):
    return x * jax.nn.sigmoid(x)


def _log_sigmoid(z):
    return jnp.minimum(z, 0.0) - jnp.log1p(jnp.exp(-jnp.abs(z)))


def _dot(a, b):
    return jnp.dot(a, b, preferred_element_type=F32)


def _dot_nt(a, b):
    return lax.dot_general(a, b, (((1,), (1,)), ((), ())), preferred_element_type=F32)


def _dot_tn(a, b):
    return lax.dot_general(a, b, (((0,), (0,)), ((), ())), preferred_element_type=F32)


def _rms(x, g):
    ms = jnp.mean(x * x, axis=-1, keepdims=True)
    return x * lax.rsqrt(ms + EPS) * g


def _mod_kernel(c_ref, w_ref, b_ref, o_ref):
    s = _silu(c_ref[...]).astype(BF16)
    o_ref[...] = _dot(s, w_ref[...]) + b_ref[...]


def _mod_call(c, w_ada_bf, b_ada):
    nb = c.shape[0]
    n_out = w_ada_bf.shape[1]
    tn = D_MODEL
    return pl.pallas_call(
        _mod_kernel,
        grid=(n_out // tn,),
        in_specs=[
            pl.BlockSpec((nb, D_MODEL), lambda j: (0, 0)),
            pl.BlockSpec((D_MODEL, tn), lambda j: (0, j)),
            pl.BlockSpec((1, tn), lambda j: (0, j)),
        ],
        out_specs=pl.BlockSpec((nb, tn), lambda j: (0, j)),
        out_shape=jax.ShapeDtypeStruct((nb, n_out), F32),
        compiler_params=_cparams(("arbitrary",)),
        name="adaln_mod",
    )(c, w_ada_bf, b_ada.reshape(1, n_out))


def _modulated_norm(x, mod_ref, g, which):
    base = 3 * which * D_MODEL
    shift = mod_ref[:, base:base + D_MODEL]
    scale = mod_ref[:, base + D_MODEL:base + 2 * D_MODEL]
    return _rms(x, g) * (1.0 + scale) + shift


def _head_norm(z, gsum, g):
    ss = _dot((z * z).astype(BF16), gsum)
    return z * lax.rsqrt(ss * (1.0 / A_HEAD_DIM) + EPS) * g


def _inproj_kernel(prompt, x_ref, mod_ref, g1_ref, w_ref, gsum_ref, qn_ref, kn_ref,
                   wgk_ref, bgk_ref, *outs):
    if prompt:
        (qT_ref, kh_ref, vT_ref, kmean_ref, kf_ref, vf_ref,
         qg_ref, kg_ref, vg_ref, og_ref, la_ref) = outs
    else:
        (q_ref, kf_ref, vf_ref, qg_ref, kg_ref, vg_ref, og_ref, la_ref) = outs
    x = x_ref[...]
    hb = _modulated_norm(x, mod_ref, g1_ref[...], 0).astype(BF16)

    qkv = _dot(hb, w_ref[:, 0:_C_QKV])
    gsum = gsum_ref[...]
    q = _head_norm(qkv[:, 0:A_WIDTH], gsum, qn_ref[...]) * (A_HEAD_DIM ** -0.5)
    k = _head_norm(qkv[:, A_WIDTH:2 * A_WIDTH], gsum, kn_ref[...])
    v = qkv[:, 2 * A_WIDTH:3 * A_WIDTH]
    kf_ref[...] = k
    vf_ref[...] = v
    if prompt:
        qT_ref[...] = q.T.astype(BF16)
        vT_ref[...] = v.T.astype(BF16)
        kb = k.astype(BF16)
        for hd in range(A_HEADS):
            kh_ref[hd] = kb[:, hd * A_HEAD_DIM:(hd + 1) * A_HEAD_DIM]
        kmean_ref[...] = jnp.sum(k, axis=0, keepdims=True) * (1.0 / MOBA_BLOCK)
    else:
        q_ref[...] = q

    qk_g = _dot(hb, w_ref[:, _C_QKV:_C_QKG])
    qg_ref[...] = qk_g[:, 0:G_KEY_DIM] * (G_DK ** -0.5)
    kg_ref[...] = qk_g[:, G_KEY_DIM:2 * G_KEY_DIM]
    vg_ref[...] = _dot(hb, w_ref[:, _C_QKG:_C_VG]).astype(BF16)
    og_ref[...] = _dot(hb, w_ref[:, _C_VG:_C_OG])
    gk_lr = _dot(hb, w_ref[:, _C_OG:_C_GK]).astype(BF16)
    z = _dot(gk_lr, wgk_ref[...]) + bgk_ref[...]
    la_ref[...] = _log_sigmoid(z) * (1.0 / G_GATE_NORMALIZER)


def _mod_spec(tm, rows_per_mod):
    if rows_per_mod == 1:
        return pl.BlockSpec((tm, 6 * D_MODEL), lambda i: (i, 0))
    tiles = rows_per_mod // tm
    return pl.BlockSpec((None, 1, 6 * D_MODEL), lambda i: (i // tiles, 0, 0))


def _inproj_call(x2, mod, rows_per_mod, tm, prompt, p):
    n = x2.shape[0]
    grid = (n // tm,)
    row = lambda w: pl.BlockSpec((tm, w), lambda i: (i, 0))
    full = lambda a: pl.BlockSpec(a.shape, lambda i: (0,) * a.ndim)
    ins = [x2, mod, p["norm1_g"], p["w_in_main"], p["gsum"], p["q_norm_t"], p["k_norm_t"],
           p["w_gk2_pad"], p["b_gk"]]
    in_specs = [row(D_MODEL), _mod_spec(tm, rows_per_mod)] + [full(a) for a in ins[2:]]
    sds = jax.ShapeDtypeStruct
    gla_shapes = [sds((n, G_KEY_DIM), F32), sds((n, G_KEY_DIM), F32), sds((n, G_VAL_DIM), BF16),
                  sds((n, G_VAL_DIM), F32), sds((n, G_KEY_DIM), F32)]
    gla_specs = [row(G_KEY_DIM), row(G_KEY_DIM), row(G_VAL_DIM), row(G_VAL_DIM), row(G_KEY_DIM)]
    if prompt:
        assert tm == MOBA_BLOCK
        out_shape = [sds((A_WIDTH, n), BF16), sds((A_HEADS, n, A_HEAD_DIM), BF16),
                     sds((A_WIDTH, n), BF16), sds((n // tm, 1, A_WIDTH), F32),
                     sds((n, A_WIDTH), F32), sds((n, A_WIDTH), F32)] + gla_shapes
        out_specs = [pl.BlockSpec((A_WIDTH, tm), lambda i: (0, i)),
                     pl.BlockSpec((A_HEADS, tm, A_HEAD_DIM), lambda i: (0, i, 0)),
                     pl.BlockSpec((A_WIDTH, tm), lambda i: (0, i)),
                     pl.BlockSpec((None, 1, A_WIDTH), lambda i: (i, 0, 0)),
                     row(A_WIDTH), row(A_WIDTH)] + gla_specs
    else:
        out_shape = [sds((n, A_WIDTH), F32), sds((n, A_WIDTH), F32), sds((n, A_WIDTH), F32)] + gla_shapes
        out_specs = [row(A_WIDTH), row(A_WIDTH), row(A_WIDTH)] + gla_specs
    return pl.pallas_call(
        functools.partial(_inproj_kernel, prompt),
        grid=grid, in_specs=in_specs, out_specs=out_specs, out_shape=out_shape,
        compiler_params=_cparams(("arbitrary",)),
        name="inproj_prompt" if prompt else "inproj_sample",
    )(*ins)


def _topk_rows(g, limit, n_sel):
    rows = lax.broadcasted_iota(I32, g.shape, 0)
    g = jnp.where(rows < limit, g, -jnp.inf)
    sel = jnp.zeros(g.shape, F32)
    for r in range(n_sel):
        mx = jnp.max(g, axis=0, keepdims=True)
        idx = jnp.min(jnp.where(g == mx, rows, g.shape[0]), axis=0, keepdims=True)
        hit = (rows == idx) & (rows < jnp.where(r < limit, limit, 0))
        sel = jnp.where(hit, 1.0, sel)
        g = jnp.where(hit, -jnp.inf, g)
    return sel


def _attn_prompt_kernel(far_ref, qT_ref, kh_ref, vT_ref, km_ref, bias_ref, o_ref, oT_scr, sel_scr):
    i = pl.program_id(1)
    tq = MOBA_BLOCK

    def softmax_step(carry, s, vT_blk):
        m, l, acc = carry
        m_new = jnp.maximum(m, jnp.max(s, axis=0, keepdims=True))
        alpha = jnp.exp(m - m_new)
        p = jnp.exp(s - m_new)
        l = l * alpha + jnp.sum(p, axis=0, keepdims=True)
        acc = acc * alpha + _dot(vT_blk, p.astype(BF16))
        return m_new, l, acc

    def head_body(h, _):
        r0 = pl.multiple_of(h * A_HEAD_DIM, A_HEAD_DIM)
        qTh = qT_ref[pl.ds(r0, A_HEAD_DIM), :]
        gate = _dot(km_ref[h].astype(BF16), qTh)
        sel_scr[...] = _topk_rows(gate, i, MOBA_TOPK)
        far = far_ref[h]

        def kv(j):
            c0 = pl.multiple_of(j * tq, tq)
            return kh_ref[h, pl.ds(c0, tq), :], vT_ref[pl.ds(r0, A_HEAD_DIM), pl.ds(c0, tq)]

        k_blk, vT_blk = kv(i)
        s = _dot(k_blk, qTh) + bias_ref[h, 0]
        m0 = jnp.max(s, axis=0, keepdims=True)
        p = jnp.exp(s - m0)
        carry = (m0, jnp.sum(p, axis=0, keepdims=True), _dot(vT_blk, p.astype(BF16)))

        def prev_body(_, carry):
            k_blk, vT_blk = kv(i - 1)
            keep = sel_scr[pl.ds(i - 1, 1), :] > 0.0
            s = jnp.where(keep, _dot(k_blk, qTh) + bias_ref[h, 1], NEG)
            return softmax_step(carry, s, vT_blk)

        def far_body(j, carry):
            k_blk, vT_blk = kv(j)
            keep = sel_scr[pl.ds(j, 1), :] > 0.0
            s = jnp.where(keep, _dot(k_blk, qTh) + far, NEG)
            return softmax_step(carry, s, vT_blk)

        carry = lax.fori_loop(0, jnp.minimum(i, 1), prev_body, carry)
        m, l, acc = lax.fori_loop(0, jnp.maximum(i - 1, 0), far_body, carry)
        oT_scr[pl.ds(r0, A_HEAD_DIM), :] = acc * (1.0 / l)
        return 0

    lax.fori_loop(0, A_HEADS, head_body, 0)
    o_ref[...] = oT_scr[...].T.astype(BF16)


def _attn_prompt_call(qT, kh, vT, kmean_h, bias_t, far, batch, seq):
    n = batch * seq
    tq = MOBA_BLOCK
    nq = seq // tq
    grid_spec = pltpu.PrefetchScalarGridSpec(
        num_scalar_prefetch=0,
        grid=(batch, nq),
        in_specs=[
            pl.BlockSpec(memory_space=pltpu.SMEM),
            pl.BlockSpec((A_WIDTH, tq), lambda b, i: (0, b * nq + i)),
            pl.BlockSpec((A_HEADS, seq, A_HEAD_DIM), lambda b, i: (0, b, 0)),
            pl.BlockSpec((A_WIDTH, seq), lambda b, i: (0, b)),
            pl.BlockSpec((None, A_HEADS, nq, A_HEAD_DIM), lambda b, i: (b, 0, 0, 0)),
            pl.BlockSpec((A_HEADS, 2, tq, tq), lambda b, i: (0, 0, 0, 0)),
        ],
        out_specs=pl.BlockSpec((tq, A_WIDTH), lambda b, i: (b * nq + i, 0)),
        scratch_shapes=[pltpu.VMEM((A_WIDTH, tq), F32), pltpu.VMEM((nq, tq), F32)],
    )
    return pl.pallas_call(
        _attn_prompt_kernel,
        grid_spec=grid_spec,
        out_shape=jax.ShapeDtypeStruct((n, A_WIDTH), BF16),
        compiler_params=_cparams(("arbitrary", "arbitrary")),
        name="moba_prompt",
    )(far, qT, kh, vT, kmean_h, bias_t)


def _topk_cols(g, n_sel):
    cols = lax.broadcasted_iota(I32, g.shape, 1)
    sel = jnp.zeros(g.shape, F32)
    for _ in range(n_sel):
        mx = jnp.max(g, axis=1, keepdims=True)
        idx = jnp.min(jnp.where(g == mx, cols, g.shape[1]), axis=1, keepdims=True)
        hit = cols == idx
        sel = jnp.where(hit, 1.0, sel)
        g = jnp.where(hit, -jnp.inf, g)
    return sel


def _attn_sample_k_kernel(page, pt_ref, *refs):
    del pt_ref
    npg = PAGES_PER_STEP
    k_refs = refs[:npg]
    (qbd_ref, knew_ref, bprev_ref, bfar_ref, bown_ref, p_ref, pown_ref, l_scr, km_scr) = refs[npg:]
    s = pl.program_id(1)
    qbd = qbd_ref[...]
    ppb = MOBA_BLOCK // page
    for b in range(npg // ppb):
        ksum = jnp.zeros((1, A_WIDTH), F32)
        for p in range(b * ppb, (b + 1) * ppb):
            kp = k_refs[p][...]
            c0 = pl.multiple_of((s * npg + p) * page, page)
            l_scr[:, pl.ds(c0, page)] = _dot_nt(qbd, kp.astype(BF16))
            ksum = ksum + jnp.sum(kp, axis=0, keepdims=True)
        km_scr[pl.ds(s * (npg // ppb) + b, 1), :] = ksum * (1.0 / MOBA_BLOCK)

    @pl.when(s == pl.num_programs(1) - 1)
    def _():
        nb = km_scr.shape[0]
        blk = lambda n: slice(n * MOBA_BLOCK, (n + 1) * MOBA_BLOCK)
        gate = _dot_nt(qbd, km_scr[...].astype(BF16))
        sel = _topk_cols(gate, min(MOBA_TOPK, nb))
        bfar = bfar_ref[...]
        s_own = _dot_nt(qbd, knew_ref[...]) + bown_ref[...]
        mx = jnp.full((qbd.shape[0], MOBA_BLOCK), NEG, F32)
        for n in range(nb):
            bias = bprev_ref[...] if n == nb - 1 else bfar
            lm = jnp.where(sel[:, n:n + 1] > 0.0, l_scr[:, blk(n)] + bias, NEG)
            l_scr[:, blk(n)] = lm
            mx = jnp.maximum(mx, lm)
        m = jnp.maximum(jnp.max(mx, axis=1, keepdims=True), jnp.max(s_own, axis=1, keepdims=True))
        p_own = jnp.exp(s_own - m)
        lsum = jnp.zeros((qbd.shape[0], MOBA_BLOCK), F32)
        for n in range(nb):
            pn = jnp.exp(l_scr[:, blk(n)] - m)
            l_scr[:, blk(n)] = pn
            lsum = lsum + pn
        inv = 1.0 / (jnp.sum(lsum, axis=1, keepdims=True) + jnp.sum(p_own, axis=1, keepdims=True))
        for n in range(nb):
            p_ref[:, blk(n)] = (l_scr[:, blk(n)] * inv).astype(BF16)
        pown_ref[...] = p_own * inv


def _attn_sample_v_kernel(t_new, pt_ref, *refs):
    del pt_ref
    npg = PAGES_PER_STEP
    v_refs = refs[:npg]
    (p_ref, pown_ref, vnew_ref, o_ref, acc_scr) = refs[npg:]
    s = pl.program_id(1)
    page = v_refs[0].shape[0]

    @pl.when(s == 0)
    def _():
        acc_scr[...] = jnp.zeros(acc_scr.shape, F32)

    acc = acc_scr[...]
    for p in range(npg):
        acc = acc + _dot(p_ref[:, p * page:(p + 1) * page], v_refs[p][...].astype(BF16))
    acc_scr[...] = acc

    @pl.when(s == pl.num_programs(1) - 1)
    def _():
        tot = acc_scr[...] + _dot(pown_ref[...].astype(BF16), vnew_ref[...])
        rows = lax.broadcasted_iota(I32, tot.shape, 0)
        lanes = lax.broadcasted_iota(I32, tot.shape, 1)
        keep = (rows & (A_HEADS - 1)) == lax.shift_right_logical(lanes, 6)
        per_tok = jnp.sum(jnp.where(keep, tot, 0.0).reshape(t_new, A_HEADS, A_WIDTH), axis=1)
        o_ref[...] = jnp.concatenate(
            [per_tok, jnp.zeros((o_ref.shape[0] - t_new, A_WIDTH), F32)], axis=0)


def _attn_sample_call(qbd, k_new, v_new, cache_k, cache_v, page_table, bprev, bfar, bown, t_new):
    n_seq, rows, _ = qbd.shape
    n_pool, page, _ = cache_k.shape
    n_pages = page_table.shape[1]
    n_keys = n_pages * page
    npg = PAGES_PER_STEP
    assert n_pages % npg == 0 and MOBA_BLOCK % page == 0 and npg % (MOBA_BLOCK // page) == 0
    steps = n_pages // npg
    own_w = k_new.shape[1]

    def page_spec(p):
        return pl.BlockSpec((None, page, A_WIDTH), lambda b, s, pt: (pt[b, s * npg + p], 0, 0))

    seq3 = lambda r, w: pl.BlockSpec((None, r, w), lambda b, s, pt: (b, 0, 0))
    full = lambda a: pl.BlockSpec(a.shape, lambda b, s, pt: (0,) * a.ndim)
    probs, p_own = pl.pallas_call(
        functools.partial(_attn_sample_k_kernel, page),
        grid_spec=pltpu.PrefetchScalarGridSpec(
            num_scalar_prefetch=1,
            grid=(n_seq, steps),
            in_specs=[page_spec(p) for p in range(npg)] + [
                seq3(rows, A_WIDTH), seq3(own_w, A_WIDTH), full(bprev), full(bfar), full(bown)],
            out_specs=[seq3(rows, n_keys), seq3(rows, own_w)],
            scratch_shapes=[pltpu.VMEM((rows, n_keys), F32),
                            pltpu.VMEM((n_keys // MOBA_BLOCK, A_WIDTH), F32)],
        ),
        out_shape=[jax.ShapeDtypeStruct((n_seq, rows, n_keys), BF16),
                   jax.ShapeDtypeStruct((n_seq, rows, own_w), F32)],
        compiler_params=_cparams(("arbitrary", "arbitrary")),
        name="moba_sample_k",
    )(page_table, *([cache_k] * npg), qbd, k_new, bprev, bfar, bown)

    out_rows = 8
    return pl.pallas_call(
        functools.partial(_attn_sample_v_kernel, t_new),
        grid_spec=pltpu.PrefetchScalarGridSpec(
            num_scalar_prefetch=1,
            grid=(n_seq, steps),
            in_specs=[page_spec(p) for p in range(npg)] + [
                pl.BlockSpec((None, rows, npg * page), lambda b, s, pt: (b, 0, s)),
                seq3(rows, own_w), seq3(own_w, A_WIDTH)],
            out_specs=seq3(out_rows, A_WIDTH),
            scratch_shapes=[pltpu.VMEM((rows, A_WIDTH), F32)],
        ),
        out_shape=jax.ShapeDtypeStruct((n_seq, out_rows, A_WIDTH), F32),
        compiler_params=_cparams(("arbitrary", "arbitrary")),
        name="moba_sample_v",
    )(page_table, *([cache_v] * npg), probs, p_own, v_new)


def _gla_kernel(has_s0, chunk, t_valid, *refs):
    if has_s0:
        qg_ref, kg_ref, la_ref, vg_ref, og_ref, gn_ref, s0_ref, o_ref, s_ref = refs
    else:
        qg_ref, kg_ref, la_ref, vg_ref, og_ref, gn_ref, o_ref, s_ref = refs
    t = pl.program_id(1)
    tt = qg_ref.shape[0]

    @pl.when(t == 0)
    def _():
        s_ref[...] = s0_ref[...] if has_s0 else jnp.zeros(s_ref.shape, F32)

    ri = lax.broadcasted_iota(I32, (chunk, chunk), 0)
    ci = lax.broadcasted_iota(I32, (chunk, chunk), 1)
    causal = ri >= ci
    tril = jnp.where(causal, 1.0, 0.0).astype(F32)
    gn = gn_ref[...]

    def chunk_body(c, _):
        r0 = pl.multiple_of(c * chunk, chunk)
        rows = pl.ds(r0, chunk)
        a = la_ref[rows, :]
        q = qg_ref[rows, :]
        k = kg_ref[rows, :]
        if t_valid is not None:
            live = (lax.broadcasted_iota(I32, (chunk, 1), 0) + (t * tt + r0)) < t_valid
            a = jnp.where(live, a, 0.0)
            k = jnp.where(live, k, 0.0)
        cum = jnp.dot(tril, a, precision=lax.Precision.HIGHEST, preferred_element_type=F32)
        last = cum[chunk - 1:chunk, :]
        q_t = (q * jnp.exp(cum)).astype(BF16)
        k_t = (k * jnp.exp(-cum)).astype(BF16)
        k_end = (k * jnp.exp(last - cum)).astype(BF16)
        decay = jnp.exp(last)
        for h in range(G_HEADS):
            kl = slice(h * G_DK, (h + 1) * G_DK)
            vl = slice(h * G_DV, (h + 1) * G_DV)
            state = s_ref[h]
            v = vg_ref[rows, vl]
            inter = _dot(q_t[:, kl], state.astype(BF16))
            att = jnp.where(causal, _dot_nt(q_t[:, kl], k_t[:, kl]), 0.0)
            o = inter + _dot(att.astype(BF16), v)
            dec_t = jnp.broadcast_to(decay[:, kl], (G_DK, G_DK)).T
            dec_t = jnp.concatenate([dec_t] * (G_DV // G_DK), axis=1)
            s_ref[h] = state * dec_t + _dot_tn(k_end[:, kl], v)
            y = _rms(o, gn) * _silu(og_ref[rows, vl])
            o_ref[rows, vl] = y.astype(BF16)
        return 0

    lax.fori_loop(0, tt // chunk, chunk_body, 0)


def _gla_call(qg, kg, la, vg, og, gn, s0, batch, seq, tt, chunk, t_valid):
    n_t = seq // tt
    row = lambda w: pl.BlockSpec((tt, w), lambda b, t: (b * n_t + t, 0))
    st_spec = pl.BlockSpec((None, G_HEADS, G_DK, G_DV), lambda b, t: (b, 0, 0, 0))
    ins = [qg, kg, la, vg, og, gn]
    in_specs = [row(G_KEY_DIM), row(G_KEY_DIM), row(G_KEY_DIM), row(G_VAL_DIM), row(G_VAL_DIM),
                pl.BlockSpec(gn.shape, lambda b, t: (0, 0))]
    if s0 is not None:
        ins.append(s0)
        in_specs.append(st_spec)
    return pl.pallas_call(
        functools.partial(_gla_kernel, s0 is not None, chunk, t_valid),
        grid=(batch, n_t),
        in_specs=in_specs,
        out_specs=[row(G_VAL_DIM), st_spec],
        out_shape=[jax.ShapeDtypeStruct((batch * seq, G_VAL_DIM), BF16),
                   jax.ShapeDtypeStruct((batch, G_HEADS, G_DK, G_DV), F32)],
        compiler_params=_cparams(("arbitrary", "arbitrary")),
        name="gla_sample" if s0 is not None else "gla_prompt",
    )(*ins)


def _post_kernel(x_ref, mod_ref, oa_ref, og_ref, g1_ref, g2_ref, wg_ref, wa_ref, wb_ref, wo_ref,
                 wr_ref, br_ref, x1_ref, h2_ref, ids_ref, wts_ref):
    x = x_ref[...]
    hb = _modulated_norm(x, mod_ref, g1_ref[...], 0).astype(BF16)
    gates = _dot(hb, wg_ref[...])
    merged = (jax.nn.sigmoid(gates[:, 0:D_MODEL]) * _dot(oa_ref[...], wa_ref[...])
              + jax.nn.sigmoid(gates[:, D_MODEL:2 * D_MODEL]) * _dot(og_ref[...], wb_ref[...]))
    gate1 = mod_ref[:, 2 * D_MODEL:3 * D_MODEL]
    x1 = x + gate1 * _dot(merged.astype(BF16), wo_ref[...])
    x1_ref[...] = x1
    h2 = _modulated_norm(x1, mod_ref, g2_ref[...], 1)
    h2_ref[...] = h2

    lg = _dot(h2.astype(BF16), wr_ref[...]) + br_ref[...]
    lane = lax.broadcasted_iota(I32, lg.shape, 1)
    first = lambda mask: jnp.min(jnp.where(mask, lane, LANES), axis=1, keepdims=True)
    in_grp = lane < N_GROUPS
    gl = jnp.where(in_grp, lg, -jnp.inf)
    gmax = jnp.max(gl, axis=1, keepdims=True)
    grp = first(gl == gmax)
    p_grp = 1.0 / jnp.sum(jnp.where(in_grp, jnp.exp(lg - gmax), 0.0), axis=1, keepdims=True)
    lo = N_GROUPS + grp * EXPERTS_PER_GROUP
    el = jnp.where((lane >= lo) & (lane < lo + EXPERTS_PER_GROUP), lg, -jnp.inf)
    v1 = jnp.max(el, axis=1, keepdims=True)
    i1 = first(el == v1)
    el2 = jnp.where(lane == i1, -jnp.inf, el)
    v2 = jnp.max(el2, axis=1, keepdims=True)
    i2 = first(el2 == v2)
    e21 = jnp.exp(v2 - v1)
    w1 = p_grp / (1.0 + e21)
    w2 = p_grp * e21 / (1.0 + e21)
    ids_ref[...] = jnp.where(lane == 0, i1 - N_GROUPS, jnp.where(lane == 1, i2 - N_GROUPS, 0))
    wts_ref[...] = jnp.where(lane == 0, w1, jnp.where(lane == 1, w2, 0.0))


def _post_call(x2, mod, rows_per_mod, tm, o_a, o_g, p):
    n = x2.shape[0]
    row = lambda w: pl.BlockSpec((tm, w), lambda i: (i, 0))
    full = lambda a: pl.BlockSpec(a.shape, lambda i: (0,) * a.ndim)
    consts = [p["norm1_g"], p["norm2_g"], p["w_in_gates"], p["w_branch_a"], p["w_branch_b"],
              p["w_out"], p["w_router"], p["b_router"]]
    sds = jax.ShapeDtypeStruct
    return pl.pallas_call(
        _post_kernel,
        grid=(n // tm,),
        in_specs=[row(D_MODEL), _mod_spec(tm, rows_per_mod), row(A_WIDTH), row(G_VAL_DIM)]
        + [full(a) for a in consts],
        out_specs=[row(D_MODEL), row(D_MODEL), row(LANES), row(LANES)],
        out_shape=[sds((n, D_MODEL), F32), sds((n, D_MODEL), F32), sds((n, LANES), I32),
                   sds((n, LANES), F32)],
        compiler_params=_cparams(("arbitrary",)),
        name="post_mixer",
    )(x2, mod, o_a, o_g, *consts)


def _row_copy(src, src_row, dst, dst_row, sem):
    return pltpu.make_async_copy(src.at[pl.ds(src_row, 1), :], dst.at[pl.ds(dst_row, 1), :], sem)


def _moe_kernel(be_ref, tok0_ref, tokn_ref, dst_ref, h_hbm, w_ref, wg_ref, wu_ref, wd_ref, y_hbm,
                xbuf, ybuf, gsem, ssem):
    del be_ref
    i = pl.program_id(0)
    nblk = pl.num_programs(0)
    slot = lax.rem(i, 2)
    other = 1 - slot

    def gather_start(tok_ref, sl):
        for r in range(MOE_BLOCK):
            _row_copy(h_hbm, tok_ref[0, r], xbuf.at[sl], r, gsem.at[sl]).start()

    def gather_wait(sl):
        for r in range(MOE_BLOCK):
            _row_copy(h_hbm, 0, xbuf.at[sl], r, gsem.at[sl]).wait()

    def scatter_start(sl):
        for r in range(MOE_BLOCK):
            _row_copy(ybuf.at[sl], r, y_hbm, dst_ref[0, r], ssem.at[sl]).start()

    def scatter_wait(sl):
        for r in range(MOE_BLOCK):
            _row_copy(ybuf.at[sl], r, y_hbm, 0, ssem.at[sl]).wait()

    @pl.when(i == 0)
    def _():
        gather_start(tok0_ref, slot)

    @pl.when(i + 1 < nblk)
    def _():
        gather_start(tokn_ref, other)

    gather_wait(slot)
    x = xbuf[slot].astype(BF16)
    hmid = (_silu(_dot(x, wg_ref[...])) * _dot(x, wu_ref[...])).astype(BF16)
    y = _dot(hmid, wd_ref[...]) * w_ref[...]

    @pl.when(i >= 2)
    def _():
        scatter_wait(slot)

    ybuf[slot] = y
    scatter_start(slot)

    @pl.when(i == nblk - 1)
    def _():
        @pl.when(nblk >= 2)
        def _():
            scatter_wait(other)
        scatter_wait(slot)


def _moe_call(h2, row_tok, row_dst, row_w, blk_expert, wg, wu, wd, n_slots_rows):
    n_blk = blk_expert.shape[0]
    smem_row = lambda f: pl.BlockSpec((None, 1, MOE_BLOCK), f, memory_space=pltpu.SMEM)
    wspec = lambda r, c: pl.BlockSpec((None, r, c), lambda i, be: (be[i], 0, 0))
    return pl.pallas_call(
        _moe_kernel,
        grid_spec=pltpu.PrefetchScalarGridSpec(
            num_scalar_prefetch=1,
            grid=(n_blk,),
            in_specs=[
                smem_row(lambda i, be: (i, 0, 0)),
                smem_row(lambda i, be: (jnp.minimum(i + 1, n_blk - 1), 0, 0)),
                smem_row(lambda i, be: (i, 0, 0)),
                pl.BlockSpec(memory_space=pl.ANY),
                pl.BlockSpec((None, MOE_BLOCK, 1), lambda i, be: (i, 0, 0)),
                wspec(D_MODEL, EXPERT_HIDDEN), wspec(D_MODEL, EXPERT_HIDDEN),
                wspec(EXPERT_HIDDEN, D_MODEL),
            ],
            out_specs=pl.BlockSpec(memory_space=pl.ANY),
            scratch_shapes=[pltpu.VMEM((2, MOE_BLOCK, D_MODEL), F32),
                            pltpu.VMEM((2, MOE_BLOCK, D_MODEL), F32),
                            pltpu.SemaphoreType.DMA((2,)), pltpu.SemaphoreType.DMA((2,))],
        ),
        out_shape=jax.ShapeDtypeStruct((n_slots_rows, D_MODEL), F32),
        compiler_params=_cparams(("arbitrary",)),
        name="moe_experts",
    )(blk_expert, row_tok, row_tok, row_dst, h2, row_w, wg, wu, wd)


def _route_tables(ids, wts, n):
    n_assign = n * EXPERT_TOPK
    flat_e = ids.reshape(-1)
    onehot = (flat_e[:, None] == jnp.arange(N_EXPERTS, dtype=I32)[None, :]).astype(I32)
    rank = jnp.take_along_axis(jnp.cumsum(onehot, axis=0), flat_e[:, None], axis=1)[:, 0] - 1
    counts = jnp.sum(onehot, axis=0)
    padded = (counts + MOE_BLOCK - 1) // MOE_BLOCK * MOE_BLOCK
    pad_ends = jnp.cumsum(padded)
    pad_starts = pad_ends - padded
    dest = pad_starts[flat_e] + rank
    n_rows = (n_assign + N_EXPERTS * (MOE_BLOCK - 1) + MOE_BLOCK - 1) // MOE_BLOCK * MOE_BLOCK
    n_blk = n_rows // MOE_BLOCK
    a_idx = jnp.arange(n_assign, dtype=I32)
    tok = a_idx // EXPERT_TOPK
    home = (a_idx % EXPERT_TOPK) * n + tok
    r_idx = jnp.arange(n_rows, dtype=I32)
    dump = EXPERT_TOPK * n + ((r_idx // MOE_BLOCK) % 2) * MOE_BLOCK + r_idx % MOE_BLOCK
    row_tok = jnp.zeros((n_rows,), I32).at[dest].set(tok)
    row_dst = dump.at[dest].set(home)
    row_w = jnp.zeros((n_rows,), F32).at[dest].set(wts.reshape(-1))
    blk_start = jnp.arange(n_blk, dtype=I32) * MOE_BLOCK
    blk_expert = jnp.minimum(jnp.searchsorted(pad_ends, blk_start, side="right"), N_EXPERTS - 1).astype(I32)
    return (row_tok.reshape(n_blk, 1, MOE_BLOCK), row_dst.reshape(n_blk, 1, MOE_BLOCK),
            row_w.reshape(n_blk, MOE_BLOCK, 1), blk_expert)


def _final_kernel(x1_ref, mod_ref, y0_ref, y1_ref, o_ref):
    gate2 = mod_ref[:, 5 * D_MODEL:6 * D_MODEL]
    o_ref[...] = x1_ref[...] + gate2 * (y0_ref[...] + y1_ref[...])


def _final_call(x1, mod, rows_per_mod, tm, y_slots):
    n = x1.shape[0]
    nt = n // tm
    row = pl.BlockSpec((tm, D_MODEL), lambda i: (i, 0))
    return pl.pallas_call(
        _final_kernel,
        grid=(nt,),
        in_specs=[row, _mod_spec(tm, rows_per_mod), row,
                  pl.BlockSpec((tm, D_MODEL), lambda i: (i + nt, 0))],
        out_specs=row,
        out_shape=jax.ShapeDtypeStruct((n, D_MODEL), F32),
        compiler_params=_cparams(("arbitrary",)),
        name="moe_combine",
    )(x1, mod, y_slots, y_slots)


def _moe(h2, ids, wts, p):
    n = h2.shape[0]
    row_tok, row_dst, row_w, blk_expert = _route_tables(ids[:, :EXPERT_TOPK], wts[:, :EXPERT_TOPK], n)
    return _moe_call(h2, row_tok, row_dst, row_w, blk_expert, p["w_exp_gate"], p["w_exp_up"],
                     p["w_exp_down"], EXPERT_TOPK * n + 2 * MOE_BLOCK)


def _t5_bucket(dist):
    n = jnp.maximum(dist, 0)
    max_exact = REL_BUCKETS // 2
    nf = jnp.maximum(n, 1).astype(F32)
    large = max_exact + (jnp.log(nf / max_exact) / math.log(REL_MAX_DIST / max_exact)
                         * (REL_BUCKETS - max_exact)).astype(I32)
    return jnp.where(n < max_exact, n, jnp.minimum(large, REL_BUCKETS - 1))


def _pack_params(layer, rel_bias, norm1_g, norm2_g, w_in, w_gk2, b_gk, q_norm_g, k_norm_g, o_norm_g,
                 w_branch_a, w_branch_b, w_out, w_router_group, b_router_group, w_router_expert,
                 b_router_expert, w_exp_gate, w_exp_up, w_exp_down):
    w = w_in[layer]
    o_gk = 3 * A_WIDTH + 2 * G_KEY_DIM + 2 * G_VAL_DIM
    gk_pad = jnp.pad(w[:, o_gk:o_gk + G_GATE_RANK], ((0, 0), (0, LANES - G_GATE_RANK)))
    head = jnp.arange(A_WIDTH) // A_HEAD_DIM
    n_rt = N_GROUPS + N_EXPERTS
    return {
        "norm1_g": norm1_g[layer].reshape(1, D_MODEL),
        "norm2_g": norm2_g[layer].reshape(1, D_MODEL),
        "w_in_main": jnp.concatenate([w[:, :o_gk], gk_pad], axis=1).astype(BF16),
        "w_in_gates": w[:, o_gk + G_GATE_RANK:].astype(BF16),
        "gsum": (head[:, None] == head[None, :]).astype(BF16),
        "q_norm_t": jnp.tile(q_norm_g[layer], A_HEADS).reshape(1, A_WIDTH),
        "k_norm_t": jnp.tile(k_norm_g[layer], A_HEADS).reshape(1, A_WIDTH),
        "w_gk2_pad": jnp.pad(w_gk2[layer], ((0, LANES - G_GATE_RANK), (0, 0))).astype(BF16),
        "b_gk": b_gk[layer].reshape(1, G_KEY_DIM),
        "o_norm_g": o_norm_g[layer].reshape(1, G_DV),
        "w_branch_a": w_branch_a[layer].astype(BF16),
        "w_branch_b": w_branch_b[layer].astype(BF16),
        "w_out": w_out[layer].astype(BF16),
        "w_router": jnp.pad(jnp.concatenate([w_router_group[layer], w_router_expert[layer]], axis=1),
                            ((0, 0), (0, LANES - n_rt))).astype(BF16),
        "b_router": jnp.pad(jnp.concatenate([b_router_group[layer], b_router_expert[layer]]),
                            (0, LANES - n_rt)).reshape(1, LANES),
        "w_exp_gate": w_exp_gate[layer].astype(BF16),
        "w_exp_up": w_exp_up[layer].astype(BF16),
        "w_exp_down": w_exp_down[layer].astype(BF16),
        "bias_by_dist": rel_bias[_t5_bucket(jnp.arange(2 * MOBA_BLOCK))].astype(F32),
        "bias_far": rel_bias[REL_BUCKETS - 1].astype(F32),
    }


def _tail(x2, mod, rows_per_mod, tm, o_a, o_g, p):
    x1, h2, ids, wts = _post_call(x2, mod, rows_per_mod, tm, o_a, o_g, p)
    y_slots = _moe(h2, ids, wts, p)
    return _final_call(x1, mod, rows_per_mod, tm, y_slots)


def _prompt_pass(x, mod, p):
    batch, seq, _ = x.shape
    n = batch * seq
    tm = MOBA_BLOCK
    x2 = x.reshape(n, D_MODEL)
    mod3 = mod.reshape(batch, 1, 6 * D_MODEL)
    (qT, kh, vT, kmean, kf, vf, qg, kg, vg, og, la) = _inproj_call(x2, mod3, seq, tm, True, p)
    nq = seq // MOBA_BLOCK
    kmean_h = kmean.reshape(batch, nq, A_HEADS, A_HEAD_DIM).transpose(0, 2, 1, 3)
    key = jnp.arange(MOBA_BLOCK)[:, None]
    qry = jnp.arange(MOBA_BLOCK)[None, :]
    tbl = p["bias_by_dist"]
    own = jnp.where((qry >= key)[..., None], tbl[jnp.maximum(qry - key, 0)], NEG)
    prev = tbl[MOBA_BLOCK + qry - key]
    bias_t = jnp.stack([own, prev], axis=0).transpose(3, 0, 1, 2)
    o_a = _attn_prompt_call(qT, kh, vT, kmean_h, bias_t, p["bias_far"], batch, seq)
    tt = min(seq, 512)
    o_g, s_new = _gla_call(qg, kg, la, vg, og, p["o_norm_g"], None, batch, seq, tt,
                           math.gcd(tt, GLA_CHUNK), None)
    y = _tail(x2, mod3, seq, tm, o_a, o_g, p)
    return (y.reshape(batch, seq, D_MODEL), kf.reshape(batch, seq, A_HEADS, A_HEAD_DIM),
            vf.reshape(batch, seq, A_HEADS, A_HEAD_DIM), s_new)


def _sample_pass(x, mod, s0, cache_k, cache_v, page_table, p):
    batch, t_new, _ = x.shape
    n = batch * t_new
    x2 = x.reshape(n, D_MODEL)
    mod_rows = jnp.repeat(mod, t_new, axis=0)
    (q, kf, vf, qg, kg, vg, og, la) = _inproj_call(x2, mod_rows, 1, n, False, p)

    n_pool, page = cache_k.shape[0], cache_k.shape[1]
    past = page_table.shape[1] * page
    assert past % MOBA_BLOCK == 0 and t_new <= 8
    rows = t_new * A_HEADS
    own_w = LANES
    q4 = q.reshape(batch, t_new, A_HEADS, 1, A_HEAD_DIM)
    eye = jnp.eye(A_HEADS, dtype=F32)[None, None, :, :, None]
    qbd = (q4 * eye).reshape(batch, rows, A_WIDTH).astype(BF16)
    pad_own = lambda z: jnp.pad(z.reshape(batch, t_new, A_WIDTH),
                                ((0, 0), (0, own_w - t_new), (0, 0))).astype(BF16)
    tbl = p["bias_by_dist"]
    t_of = jnp.repeat(jnp.arange(t_new), A_HEADS)
    h_of = jnp.tile(jnp.arange(A_HEADS), t_new)
    key = jnp.arange(MOBA_BLOCK)[None, :]
    bprev = tbl[MOBA_BLOCK + t_of[:, None] - key, h_of[:, None]]
    bfar = p["bias_far"][h_of].reshape(rows, 1)
    s_new_tok = jnp.arange(own_w)[None, :]
    d_own = t_of[:, None] - s_new_tok
    bown = jnp.where((d_own >= 0) & (s_new_tok < t_new),
                     tbl[jnp.clip(d_own, 0, 2 * MOBA_BLOCK - 1), h_of[:, None]], NEG)
    o8 = _attn_sample_call(qbd, pad_own(kf), pad_own(vf),
                           cache_k.reshape(n_pool, page, A_WIDTH), cache_v.reshape(n_pool, page, A_WIDTH),
                           page_table, bprev, bfar, bown, t_new)
    o_a = o8[:, :t_new].reshape(n, A_WIDTH).astype(BF16)

    t_pad = 16
    padt = lambda z: jnp.pad(z.reshape(batch, t_new, -1),
                             ((0, 0), (0, t_pad - t_new), (0, 0))).reshape(batch * t_pad, -1)
    o_g, s_new = _gla_call(padt(qg), padt(kg), padt(la), padt(vg), padt(og), p["o_norm_g"], s0,
                           batch, t_pad, t_pad, t_pad, t_new)
    o_g = o_g.reshape(batch, t_pad, G_VAL_DIM)[:, :t_new].reshape(n, G_VAL_DIM)
    y = _tail(x2, mod_rows, 1, n, o_a, o_g, p)
    return (y.reshape(batch, t_new, D_MODEL), kf.reshape(batch, t_new, A_HEADS, A_HEAD_DIM),
            vf.reshape(batch, t_new, A_HEADS, A_HEAD_DIM), s_new)


def kernel(x_prompt, x_sample, c_prompt, c_sample, cache_k, cache_v, state_gla, page_table, rel_bias,
           norm1_g, norm2_g, w_ada, b_ada, w_in, w_gk2, b_gk, q_norm_g, k_norm_g, o_norm_g,
           w_branch_a, w_branch_b, w_out, w_router_group, b_router_group, w_router_expert,
           b_router_expert, w_exp_gate, w_exp_up, w_exp_down):
    depth = w_in.shape[0]
    n_prompt = c_prompt.shape[0]
    y_p, y_s = x_prompt, x_sample
    outs = [[] for _ in range(6)]
    for layer in range(depth):
        p = _pack_params(layer, rel_bias, norm1_g, norm2_g, w_in, w_gk2, b_gk, q_norm_g, k_norm_g,
                         o_norm_g, w_branch_a, w_branch_b, w_out, w_router_group, b_router_group,
                         w_router_expert, b_router_expert, w_exp_gate, w_exp_up, w_exp_down)
        mod = _mod_call(jnp.concatenate([c_prompt, c_sample], axis=0), w_ada[layer].astype(BF16),
                        b_ada[layer])
        y_p, k_p, v_p, s_p = _prompt_pass(y_p, mod[:n_prompt], p)
        y_s, k_s, v_s, s_s = _sample_pass(y_s, mod[n_prompt:], state_gla[layer], cache_k[layer],
                                          cache_v[layer], page_table, p)
        for lst, val in zip(outs, (k_p, v_p, s_p, k_s, v_s, s_s)):
            lst.append(val)
    return (y_p, y_s) + tuple(jnp.stack(lst) for lst in outs)
```
